```python
import math
import jax, jax.numpy as jnp
from jax import lax
import numpy as np

D_MODEL = 1024
BATCH = 2
SEQ = 8192
DEPTH = 2
DEC_BATCH = 128
DEC_SEQ = 8
PAST_LEN = 2048
PAGE_SIZE = 128

N_MEM = 256
A_HEADS = 4
A_QK_DIM = 64
A_V_DIM = 128
A_WIDTH = A_HEADS * A_V_DIM
G_HEADS = 4
G_K_DIM = 64
G_V_DIM = 128
G_WIDTH = G_HEADS * G_V_DIM
G_GATE_RANK = 16
G_GATE_TAU = 16.0
M_HEADS = 4
M_DIM = 128
M_WIDTH = M_HEADS * M_DIM
BRANCH_WIDTH = 512
N_BRANCH = 3
Q_BLOCK = 128
GLA_CHUNK = 64
EPS = 1e-6
IN_SIZES = (A_HEADS * 2 * A_QK_DIM, A_HEADS * 2 * A_QK_DIM, A_WIDTH, A_WIDTH,
            G_HEADS * G_K_DIM, G_HEADS * G_K_DIM, G_WIDTH, G_WIDTH, G_GATE_RANK,
            M_WIDTH, M_WIDTH, N_BRANCH * D_MODEL)
IN_COLS = sum(IN_SIZES)

kernel_name = "hybrid_diffattn_gla_memory_decode_step"


def rmsnorm(x, g):
    xf = x.astype(jnp.float32)
    y = xf * lax.rsqrt(jnp.mean(xf * xf, axis=-1, keepdims=True) + EPS) * g.astype(jnp.float32)
    return y.astype(x.dtype)


def alibi_slopes(n_heads):
    return jnp.exp2(-8.0 * jnp.arange(1, n_heads + 1, dtype=jnp.float32) / n_heads)


def diff_attention(q, k, v, q_pos, k_pos, lam, block):
    B, Tq, H, _, dk = q.shape
    nb = Tq // block
    scale = dk ** -0.5
    slopes = alibi_slopes(H)
    kf = k.astype(jnp.float32)
    vf = v.astype(jnp.float32)
    qb = q.reshape(B, nb, block, H, 2, dk).transpose(1, 0, 2, 3, 4, 5)
    pb = q_pos.reshape(nb, block)

    def one_block(args):
        qi, pi = args
        s = jnp.einsum('bqhcd,bkhcd->bhcqk', qi.astype(jnp.float32), kf) * scale
        dist = pi[:, None] - k_pos[None, :]
        bias = -slopes[:, None, None] * dist.astype(jnp.float32)[None]
        s = jnp.where((dist >= 0)[None, None, None], s + bias[None, :, None], -jnp.inf)
        p = jax.nn.softmax(s, axis=-1)
        a = p[:, :, 0] - lam * p[:, :, 1]
        return jnp.einsum('bhqk,bkhd->bqhd', a, vf)

    out = lax.map(one_block, (qb, pb))
    return out.transpose(1, 0, 2, 3, 4).reshape(B, Tq, H, v.shape[-1]).astype(v.dtype)


def gla_chunked(q, k, v, log_a, S0, chunk):
    B, T, H, dk = q.shape
    dv = v.shape[-1]
    nc = T // chunk

    def to_chunks(t):
        return t.astype(jnp.float32).reshape(B, nc, chunk, H, t.shape[-1]).transpose(1, 0, 3, 2, 4)

    tril = jnp.tril(jnp.ones((chunk, chunk), dtype=bool))

    def step(S, inp):
        qc, kc, vc, lc = inp
        b = jnp.cumsum(lc, axis=2)
        diff = b[:, :, :, None, :] - b[:, :, None, :, :]
        dec = jnp.exp(jnp.where(tril[:, :, None], diff, -jnp.inf))
        att = jnp.einsum('bhtd,bhtsd,bhsd->bhts', qc, dec, kc)
        o = jnp.einsum('bhts,bhsv->bhtv', att, vc) + jnp.einsum('bhtd,bhdv->bhtv', qc * jnp.exp(b), S)
        b_last = b[:, :, -1, :]
        S = jnp.exp(b_last)[..., None] * S + jnp.einsum(
            'bhsd,bhsv->bhdv', kc * jnp.exp(b_last[:, :, None, :] - b), vc)
        return S, o

    S, o = lax.scan(step, S0.astype(jnp.float32),
                    (to_chunks(q), to_chunks(k), to_chunks(v), to_chunks(log_a)))
    o = o.transpose(1, 0, 3, 2, 4).reshape(B, T, H, dv)
    return o.astype(v.dtype), S


def memory_kv(mem, norm_mem, w_mkv, g_km):
    B, N, _ = mem.shape
    hm = rmsnorm(mem, norm_mem)
    mk, mv = jnp.split(hm @ w_mkv, 2, axis=-1)
    mk = rmsnorm(mk.reshape(B, N, M_HEADS, M_DIM), g_km)
    return mk, mv.reshape(B, N, M_HEADS, M_DIM)


def layer(x, mem_k, mem_v, past_k, past_v, S0, pos0, l,
          norm_x, w_in, g_qa, g_ka, lam, g_oa, w_a2, b_a, g_ob, g_qm, w_branch, w_o):
    B, T, _ = x.shape
    h = rmsnorm(x, norm_x)
    proj = h @ w_in
    (aq, ak, av, az, gq, gk, gv, gz, gr, mq, mz, gl) = jnp.split(
        proj, np.cumsum(IN_SIZES)[:-1].tolist(), axis=-1)

    aq = rmsnorm(aq.reshape(B, T, A_HEADS, 2, A_QK_DIM), g_qa)
    ak = rmsnorm(ak.reshape(B, T, A_HEADS, 2, A_QK_DIM), g_ka)
    av = av.reshape(B, T, A_HEADS, A_V_DIM)
    new_k = ak.reshape(B, T, A_HEADS, 2 * A_QK_DIM)
    new_v = av
    if past_k is None:
        k_all, v_all = new_k, new_v
    else:
        k_all = jnp.concatenate([past_k.astype(new_k.dtype), new_k], axis=1)
        v_all = jnp.concatenate([past_v.astype(new_v.dtype), new_v], axis=1)
    Tk = k_all.shape[1]
    lam_init = 0.8 - 0.6 * math.exp(-0.3 * l)
    lf = lam.astype(jnp.float32)
    lam_val = jnp.exp(jnp.sum(lf[0] * lf[1])) - jnp.exp(jnp.sum(lf[2] * lf[3])) + lam_init
    q_pos = pos0 + jnp.arange(T, dtype=jnp.int32)
    k_pos = jnp.arange(Tk, dtype=jnp.int32)
    block = Q_BLOCK if T % Q_BLOCK == 0 else T
    oa = diff_attention(aq, k_all.reshape(B, Tk, A_HEADS, 2, A_QK_DIM), v_all,
                        q_pos, k_pos, lam_val, block)
    oa = rmsnorm(oa, g_oa) * (1.0 - lam_init)
    ua = (oa.reshape(B, T, A_WIDTH) * jax.nn.silu(az)) @ w_branch[0]

    log_a = jax.nn.log_sigmoid((gr @ w_a2 + b_a).astype(jnp.float32)) / G_GATE_TAU
    log_a = log_a.reshape(B, T, G_HEADS, G_K_DIM)
    gq = gq.reshape(B, T, G_HEADS, G_K_DIM) * (G_K_DIM ** -0.5)
    gk = gk.reshape(B, T, G_HEADS, G_K_DIM)
    gv = gv.reshape(B, T, G_HEADS, G_V_DIM)
    chunk = GLA_CHUNK if T % GLA_CHUNK == 0 else T
    ob, S = gla_chunked(gq, gk, gv, log_a, S0, chunk)
    ob = rmsnorm(ob, g_ob)
    ub = (ob.reshape(B, T, G_WIDTH) * jax.nn.silu(gz)) @ w_branch[1]

    mq = rmsnorm(mq.reshape(B, T, M_HEADS, M_DIM), g_qm)
    s = jnp.einsum('bthd,bmhd->bhtm', mq.astype(jnp.float32),
                   mem_k.astype(jnp.float32)) * (M_DIM ** -0.5)
    p = jax.nn.softmax(s, axis=-1)
    om = jnp.einsum('bhtm,bmhd->bthd', p, mem_v.astype(jnp.float32)).astype(x.dtype)
    um = (om.reshape(B, T, M_WIDTH) * jax.nn.silu(mz)) @ w_branch[2]

    g_a, g_b, g_m = jnp.split(gl, N_BRANCH, axis=-1)
    merged = jax.nn.sigmoid(g_a) * ua + jax.nn.sigmoid(g_b) * ub + jax.nn.sigmoid(g_m) * um
    y = x + merged @ w_o
    return y, new_k, new_v, S.astype(x.dtype)


def setup_inputs(seed: int = 0) -> dict:
    key = jax.random.key(seed)
    ks = jax.random.split(key, 32)
    f32 = jnp.float32
    n_pages = PAST_LEN // PAGE_SIZE
    used = DEC_BATCH * n_pages
    n_pool = used + max(1, used // 4)
    nrm = lambda k, shape, s: jax.random.normal(k, shape, f32) * s
    gain = lambda k, shape: 1.0 + 0.02 * jax.random.normal(k, shape, f32)
    page_table = jax.random.permutation(ks[0], n_pool)[:used].reshape(DEC_BATCH, n_pages).astype(jnp.int32)
    return {
        "x_prompt": nrm(ks[1], (BATCH, SEQ, D_MODEL), 1.0),
        "x_sample": nrm(ks[2], (DEC_BATCH, DEC_SEQ, D_MODEL), 1.0),
        "cache_k_a": nrm(ks[3], (DEPTH, n_pool, PAGE_SIZE, A_HEADS, 2 * A_QK_DIM), 1.0),
        "cache_v_a": nrm(ks[4], (DEPTH, n_pool, PAGE_SIZE, A_HEADS, A_V_DIM), 1.0),
        "page_table": page_table,
        "state_gla": nrm(ks[5], (DEPTH, DEC_BATCH, G_HEADS, G_K_DIM, G_V_DIM), 0.3),
        "cache_mem_k": nrm(ks[6], (DEPTH, DEC_BATCH, N_MEM, M_HEADS, M_DIM), 1.0),
        "cache_mem_v": nrm(ks[7], (DEPTH, DEC_BATCH, N_MEM, M_HEADS, M_DIM), 1.0),
        "mem_prompt": nrm(ks[8], (BATCH, N_MEM, D_MODEL), 1.0),
        "norm_x": gain(ks[9], (DEPTH, D_MODEL)),
        "w_in": nrm(ks[10], (DEPTH, D_MODEL, IN_COLS), D_MODEL ** -0.5),
        "g_qa": gain(ks[11], (DEPTH, A_QK_DIM)),
        "g_ka": gain(ks[12], (DEPTH, A_QK_DIM)),
        "lam": nrm(ks[13], (DEPTH, 4, A_QK_DIM), 0.1),
        "g_oa": gain(ks[14], (DEPTH, A_V_DIM)),
        "w_a2": nrm(ks[15], (DEPTH, G_GATE_RANK, G_HEADS * G_K_DIM), G_GATE_RANK ** -0.5),
        "b_a": nrm(ks[16], (DEPTH, G_HEADS * G_K_DIM), 0.1),
        "g_ob": gain(ks[17], (DEPTH, G_V_DIM)),
        "g_qm": gain(ks[18], (DEPTH, M_DIM)),
        "g_km": gain(ks[19], (DEPTH, M_DIM)),
        "norm_mem": gain(ks[20], (DEPTH, D_MODEL)),
        "w_mkv": nrm(ks[21], (DEPTH, D_MODEL, 2 * M_WIDTH), D_MODEL ** -0.5),
        "w_branch": nrm(ks[22], (DEPTH, N_BRANCH, BRANCH_WIDTH, D_MODEL), BRANCH_WIDTH ** -0.5),
        "w_o": nrm(ks[23], (DEPTH, D_MODEL, D_MODEL), D_MODEL ** -0.5),
    }


def reference(x_prompt, x_sample, cache_k_a, cache_v_a, page_table, state_gla,
              cache_mem_k, cache_mem_v, mem_prompt, norm_x, w_in, g_qa, g_ka, lam, g_oa,
              w_a2, b_a, g_ob, g_qm, g_km, norm_mem, w_mkv, w_branch, w_o):
    Bp = x_prompt.shape[0]
    Bs = x_sample.shape[0]
    n_pages = page_table.shape[1]
    page = cache_k_a.shape[2]
    past_len = n_pages * page

    x = x_prompt
    kp, vp, sp, mkp, mvp = [], [], [], [], []
    for l in range(DEPTH):
        mk, mv = memory_kv(mem_prompt, norm_mem[l], w_mkv[l], g_km[l])
        S0 = jnp.zeros((Bp, G_HEADS, G_K_DIM, G_V_DIM), jnp.float32)
        x, nk, nv, S = layer(x, mk, mv, None, None, S0, 0, l,
                             norm_x[l], w_in[l], g_qa[l], g_ka[l], lam[l], g_oa[l],
                             w_a2[l], b_a[l], g_ob[l], g_qm[l], w_branch[l], w_o[l])
        kp.append(nk); vp.append(nv); sp.append(S); mkp.append(mk); mvp.append(mv)
    y_prompt = x

    x = x_sample
    ks_, vs_, ss_ = [], [], []
    for l in range(DEPTH):
        past_k = cache_k_a[l][page_table].reshape(Bs, past_len, A_HEADS, 2 * A_QK_DIM)
        past_v = cache_v_a[l][page_table].reshape(Bs, past_len, A_HEADS, A_V_DIM)
        x, nk, nv, S = layer(x, cache_mem_k[l], cache_mem_v[l], past_k, past_v,
                             state_gla[l], past_len, l,
                             norm_x[l], w_in[l], g_qa[l], g_ka[l], lam[l], g_oa[l],
                             w_a2[l], b_a[l], g_ob[l], g_qm[l], w_branch[l], w_o[l])
        ks_.append(nk); vs_.append(nv); ss_.append(S)
    y_sample = x

    return (y_prompt, y_sample,
            jnp.stack(kp), jnp.stack(vp), jnp.stack(ks_), jnp.stack(vs_),
            jnp.stack(sp), jnp.stack(ss_), jnp.stack(mkp), jnp.stack(mvp))
```

```python
import functools
import math

import numpy as np
import jax
import jax.numpy as jnp
from jax import lax
from jax.experimental import pallas as pl
from jax.experimental.pallas import tpu as pltpu

F32 = jnp.float32
BF16 = jnp.bfloat16

EPS = 1e-6
HEADS = 4
A_QK = 64
HEAD_W = 128
WIDTH = HEADS * HEAD_W
G_K = 64
G_KW = HEADS * G_K
LOG2_G_K = 6
LOG2_HEAD_W = 7
G_RANK = 16
G_TAU = 16.0
N_BRANCH = 3
LANE = 128
NEG = -1e30
VMEM_LIMIT = 56 * 1024 * 1024

OFF_AQ, OFF_AK, OFF_AV, OFF_AZ = 0, 512, 1024, 1536
OFF_GQ, OFF_GK, OFF_GV, OFF_GZ = 2048, 2304, 2560, 3072
OFF_GR = 3584
OFF_MQ, OFF_MZ, OFF_GL = 3712, 4224, 4736
W_COLS = 7808
SRC_GR_END = 3600


def _dot(a, b):
    return jnp.dot(a, b, preferred_element_type=F32)


def _dot_nt(a, b):
    return lax.dot_general(a, b, (((1,), (1,)), ((), ())), preferred_element_type=F32)


def _dot_tn(a, b):
    return lax.dot_general(a, b, (((0,), (0,)), ((), ())), preferred_element_type=F32)


def _split3(x):
    hi = x.astype(BF16)
    r = x - hi.astype(F32)
    mid = r.astype(BF16)
    lo = (r - mid.astype(F32)).astype(BF16)
    return hi, mid, lo


def _rms(t, g):
    return t * lax.rsqrt(jnp.mean(t * t, axis=-1, keepdims=True) + EPS) * g


def _params(sem):
    return pltpu.CompilerParams(dimension_semantics=sem, vmem_limit_bytes=VMEM_LIMIT)


def _const_spec(shape):
    nd = len(shape)
    return pl.BlockSpec(shape, lambda *_: (0,) * nd, pipeline_mode=pl.Buffered(1))


def _inproj_kernel(x_ref, nx_ref, w_ref, gm_ref, gqa_ref, gka_ref, gqm_ref, wa2_ref, ba_ref,
                   q_ref, k_ref, v_ref, kb_ref, vb_ref, az_ref, gq_ref, gk_ref, gv_ref, gz_ref,
                   la_ref, mq_ref, mz_ref, gl_ref):
    x = x_ref[...]
    hb = _rms(x, nx_ref[...]).astype(BF16)
    gm = gm_ref[...]

    def seg(off, n):
        return _dot(hb, w_ref[:, off:off + n])

    def norm64(t, g):
        sq = t * t
        hi = sq.astype(BF16)
        lo = (sq - hi.astype(F32)).astype(BF16)
        ms = _dot(hi, gm) + _dot(lo, gm)
        return t * lax.rsqrt(ms + EPS) * g

    q_ref[...] = (norm64(seg(OFF_AQ, WIDTH), gqa_ref[...]) * (A_QK ** -0.5)).astype(q_ref.dtype)
    kn = norm64(seg(OFF_AK, WIDTH), gka_ref[...])
    k_ref[...] = kn
    kb_ref[...] = kn.astype(kb_ref.dtype)
    av = seg(OFF_AV, WIDTH)
    v_ref[...] = av
    vb_ref[...] = av.astype(vb_ref.dtype)
    az_ref[...] = seg(OFF_AZ, WIDTH)
    gq_ref[...] = seg(OFF_GQ, G_KW) * (G_K ** -0.5)
    gk_ref[...] = seg(OFF_GK, G_KW)
    gv_ref[...] = seg(OFF_GV, WIDTH).astype(gv_ref.dtype)
    gz_ref[...] = seg(OFF_GZ, WIDTH)
    gr = seg(OFF_GR, LANE)
    z = _dot(gr.astype(BF16), wa2_ref[...]) + ba_ref[...]
    la_ref[...] = (jnp.minimum(z, 0.0) - jnp.log(1.0 + jnp.exp(-jnp.abs(z)))) * (1.0 / G_TAU)
    mq = seg(OFF_MQ, WIDTH)
    gqm = gqm_ref[...]
    for hd in range(HEADS):
        sl = slice(hd * HEAD_W, (hd + 1) * HEAD_W)
        mq_ref[:, sl] = (_rms(mq[:, sl], gqm) * (HEAD_W ** -0.5)).astype(mq_ref.dtype)
    mz_ref[...] = seg(OFF_MZ, WIDTH)
    for c in range(0, N_BRANCH * 1024, 512):
        gl_ref[:, c:c + 512] = seg(OFF_GL + c, 512)


def _inproj(x2d, wts, act_dtype, tm):
    n, d = x2d.shape
    rows = lambda c: pl.BlockSpec((tm, c), lambda i: (i, 0))
    out_cols = (WIDTH, WIDTH, WIDTH, WIDTH, WIDTH, WIDTH, G_KW, G_KW, WIDTH, WIDTH, G_KW, WIDTH,
                WIDTH, N_BRANCH * d)
    out_dt = (act_dtype, F32, F32, BF16, BF16, F32, F32, F32, act_dtype, F32, F32, act_dtype,
              F32, F32)
    consts = (wts["norm_x"], wts["w_in"], wts["gmat"], wts["g_qa"], wts["g_ka"], wts["g_qm"],
              wts["w_a2"], wts["b_a"])
    return pl.pallas_call(
        _inproj_kernel,
        grid=(n // tm,),
        in_specs=[rows(d)] + [_const_spec(c.shape) for c in consts],
        out_specs=[rows(c) for c in out_cols],
        out_shape=[jax.ShapeDtypeStruct((n, c), dt) for c, dt in zip(out_cols, out_dt)],
        compiler_params=_params(("arbitrary",)),
        name="inproj",
    )(x2d, *consts)


def _lam_value(lam_ref, lam_init):
    lm = lam_ref[...]
    a = jnp.sum(lm[0:1] * lm[1:2], axis=-1, keepdims=True)
    b = jnp.sum(lm[2:3] * lm[3:4], axis=-1, keepdims=True)
    return jnp.exp(a) - jnp.exp(b) + lam_init


def _stack_maps(q):
    lane = lax.broadcasted_iota(jnp.int32, q.shape, 1)
    zero = jnp.zeros_like(q)
    return jnp.concatenate([jnp.where(lane < A_QK, q, zero), jnp.where(lane >= A_QK, q, zero)], 0)


def _attn_prompt_kernel(qt_ref, kt_ref, slope_ref, q_ref, k_ref, v_ref, lam_ref, goa_ref, o_ref,
                        qs_ref, m_ref, l_ref, acc_ref, *, tq, tk, lam_init):
    hd = pl.program_id(1)
    p = pl.program_id(2)
    qi = qt_ref[p]
    ki = kt_ref[p]

    @pl.when(ki == 0)
    def _init():
        qs_ref[...] = _stack_maps(q_ref[...])
        m_ref[...] = jnp.full(m_ref.shape, NEG, F32)
        l_ref[...] = jnp.zeros(l_ref.shape, F32)
        acc_ref[...] = jnp.zeros(acc_ref.shape, F32)

    s = _dot_nt(qs_ref[...], k_ref[...])
    dist = (lax.broadcasted_iota(jnp.int32, (tq, tk), 0)
            - lax.broadcasted_iota(jnp.int32, (tq, tk), 1)) + (qi * tq - ki * tk)
    bias = dist.astype(F32) * (-slope_ref[hd])
    keep = dist >= 0
    v = v_ref[...]
    for c in range(2):
        rows = slice(c * tq, (c + 1) * tq)
        sc = jnp.where(keep, s[rows] + bias, NEG)
        m_prev = m_ref[rows]
        m_new = jnp.maximum(m_prev, jnp.max(sc, axis=-1, keepdims=True))
        alpha = jnp.exp(m_prev - m_new)
        pm = jnp.exp(sc - m_new)
        l_ref[rows] = alpha * l_ref[rows] + jnp.sum(pm, axis=-1, keepdims=True)
        acc_ref[rows] = alpha * acc_ref[rows] + _dot(pm.astype(BF16), v)
        m_ref[rows] = m_new

    @pl.when(ki == (qi * tq + tq - 1) // tk)
    def _finish():
        lam_val = _lam_value(lam_ref, lam_init)
        o = acc_ref[0:tq] / l_ref[0:tq] - lam_val * (acc_ref[tq:] / l_ref[tq:])
        o_ref[...] = _rms(o, goa_ref[...]) * (1.0 - lam_init)


def _attn_prompt(q, kb, vb, lam, g_oa, lam_init, tq, tk):
    b, t, _ = q.shape
    nq = t // tq
    pairs = [(i, j) for i in range(nq) for j in range((i * tq + tq - 1) // tk + 1)]
    qtab = jnp.asarray(np.array([pq for pq, _ in pairs], np.int32))
    ktab = jnp.asarray(np.array([pk for _, pk in pairs], np.int32))
    slopes = jnp.asarray(np.exp2(-8.0 * np.arange(1, HEADS + 1) / HEADS).astype(np.float32))
    grid_spec = pltpu.PrefetchScalarGridSpec(
        num_scalar_prefetch=2,
        grid=(b, HEADS, len(pairs)),
        in_specs=[
            pl.BlockSpec(memory_space=pltpu.SMEM),
            pl.BlockSpec((None, tq, HEAD_W), lambda bb, h, p, qt, kt: (bb, qt[p], h)),
            pl.BlockSpec((None, tk, HEAD_W), lambda bb, h, p, qt, kt: (bb, kt[p], h)),
            pl.BlockSpec((None, tk, HEAD_W), lambda bb, h, p, qt, kt: (bb, kt[p], h)),
            pl.BlockSpec(lam.shape, lambda bb, h, p, qt, kt: (0, 0)),
            pl.BlockSpec(g_oa.shape, lambda bb, h, p, qt, kt: (0, 0)),
        ],
        out_specs=pl.BlockSpec((None, tq, HEAD_W), lambda bb, h, p, qt, kt: (bb, qt[p], h)),
        scratch_shapes=[
            pltpu.VMEM((2 * tq, HEAD_W), BF16),
            pltpu.VMEM((2 * tq, 1), F32),
            pltpu.VMEM((2 * tq, 1), F32),
            pltpu.VMEM((2 * tq, HEAD_W), F32),
        ],
    )
    return pl.pallas_call(
        functools.partial(_attn_prompt_kernel, tq=tq, tk=tk, lam_init=lam_init),
        grid_spec=grid_spec,
        out_shape=jax.ShapeDtypeStruct((b, t, WIDTH), F32),
        compiler_params=_params(("arbitrary", "arbitrary", "arbitrary")),
        name="attn_prompt",
    )(qtab, ktab, slopes, q, kb, vb, lam, g_oa)


def _attn_sample_kernel(pt_ref, slope_ref, q_ref, kn_ref, vn_ref, lam_ref, goa_ref, kc_ref, vc_ref,
                        o_ref, kbuf, vbuf, sem, *, n_pages, page, t_new, lam_init):
    b = pl.program_id(0)
    nb = pl.num_programs(0)
    slot = b % 2

    def page_copies(bb, sl):
        out = []
        for pg in range(n_pages):
            pid = pt_ref[bb, pg]
            out.append(pltpu.make_async_copy(kc_ref.at[pid], kbuf.at[sl, pg], sem.at[0, sl]))
            out.append(pltpu.make_async_copy(vc_ref.at[pid], vbuf.at[sl, pg], sem.at[1, sl]))
        return out

    @pl.when(b == 0)
    def _first():
        zeros = jnp.zeros((page, WIDTH), F32)
        for sl in range(2):
            kbuf[sl, n_pages] = zeros
            vbuf[sl, n_pages] = zeros
        for cp in page_copies(b, slot):
            cp.start()

    @pl.when(b + 1 < nb)
    def _prefetch():
        for cp in page_copies(b + 1, 1 - slot):
            cp.start()

    kbuf[slot, n_pages, 0:t_new, :] = kn_ref[...]
    vbuf[slot, n_pages, 0:t_new, :] = vn_ref[...]
    for cp in page_copies(b, slot):
        cp.wait()

    n_keys = (n_pages + 1) * page
    past = n_pages * page
    dist = (lax.broadcasted_iota(jnp.int32, (t_new, n_keys), 0)
            - lax.broadcasted_iota(jnp.int32, (t_new, n_keys), 1)) + past
    dist = jnp.concatenate([dist, dist], axis=0)
    keep = dist >= 0
    distf = dist.astype(F32)
    lam_val = _lam_value(lam_ref, lam_init)
    for hd in range(HEADS):
        sl = slice(hd * HEAD_W, (hd + 1) * HEAD_W)
        kh = kbuf[slot, :, :, sl].reshape(n_keys, HEAD_W).astype(BF16)
        vh = vbuf[slot, :, :, sl].reshape(n_keys, HEAD_W).astype(BF16)
        qs = _stack_maps(q_ref[:, sl]).astype(BF16)
        s = _dot_nt(qs, kh) - distf * slope_ref[hd]
        s = jnp.where(keep, s, NEG)
        m = jnp.max(s, axis=-1, keepdims=True)
        pm = jnp.exp(s - m)
        o = _dot(pm.astype(BF16), vh) / jnp.sum(pm, axis=-1, keepdims=True)
        o = o[0:t_new] - lam_val * o[t_new:]
        o_ref[:, sl] = _rms(o, goa_ref[...]) * (1.0 - lam_init)


def _attn_sample(q, kn, vn, lam, g_oa, cache_k, cache_v, page_table, lam_init):
    b, t_new, _ = q.shape
    n_pages = page_table.shape[1]
    page = cache_k.shape[1]
    slopes = jnp.asarray(np.exp2(-8.0 * np.arange(1, HEADS + 1) / HEADS).astype(np.float32))
    row = pl.BlockSpec((None, t_new, WIDTH), lambda i, pt: (i, 0, 0))
    grid_spec = pltpu.PrefetchScalarGridSpec(
        num_scalar_prefetch=1,
        grid=(b,),
        in_specs=[
            pl.BlockSpec(memory_space=pltpu.SMEM),
            row, row, row,
            pl.BlockSpec(lam.shape, lambda i, pt: (0, 0)),
            pl.BlockSpec(g_oa.shape, lambda i, pt: (0, 0)),
            pl.BlockSpec(memory_space=pl.ANY),
            pl.BlockSpec(memory_space=pl.ANY),
        ],
        out_specs=row,
        scratch_shapes=[
            pltpu.VMEM((2, n_pages + 1, page, WIDTH), F32),
            pltpu.VMEM((2, n_pages + 1, page, WIDTH), F32),
            pltpu.SemaphoreType.DMA((2, 2)),
        ],
    )
    return pl.pallas_call(
        functools.partial(_attn_sample_kernel, n_pages=n_pages, page=page, t_new=t_new,
                          lam_init=lam_init),
        grid_spec=grid_spec,
        out_shape=jax.ShapeDtypeStruct((b, t_new, WIDTH), F32),
        compiler_params=_params(("arbitrary",)),
        name="attn_sample",
    )(page_table, slopes, q, kn, vn, lam, g_oa, cache_k, cache_v)


def _gla_kernel(gq_ref, gk_ref, gv_ref, la_ref, s0_ref, gob_ref, ob_ref, sout_ref, s_ref,
                *, chunk, rows):
    i = pl.program_id(1)
    c_rows = chunk

    @pl.when(i == 0)
    def _init():
        s_ref[...] = jnp.zeros(s_ref.shape, F32)
        for hd in range(HEADS):
            s_ref[hd * G_K:(hd + 1) * G_K, hd * HEAD_W:(hd + 1) * HEAD_W] = s0_ref[hd]

    tri = (lax.broadcasted_iota(jnp.int32, (c_rows, c_rows), 0)
           >= lax.broadcasted_iota(jnp.int32, (c_rows, c_rows), 1))
    tri_bf = jnp.where(tri, 1.0, 0.0).astype(BF16)
    ones_bf = jnp.ones((c_rows, LANE), BF16)
    head_of_lane = lax.broadcasted_iota(jnp.int32, (c_rows, G_KW), 1) >> LOG2_G_K
    diag_blocks = ((lax.broadcasted_iota(jnp.int32, (G_KW, WIDTH), 0) >> LOG2_G_K)
                   == (lax.broadcasted_iota(jnp.int32, (G_KW, WIDTH), 1) >> LOG2_HEAD_W))
    gob = gob_ref[...]

    def padded(ref, r0, n):
        t = ref[r0:r0 + n, :].astype(F32)
        if n < c_rows:
            t = jnp.concatenate([t, jnp.zeros((c_rows - n, t.shape[1]), F32)], axis=0)
        return t

    for r0 in range(0, rows, c_rows):
        n = min(c_rows, rows - r0)
        la = padded(la_ref, r0, n)
        q = padded(gq_ref, r0, n)
        k = padded(gk_ref, r0, n)
        v = padded(gv_ref, r0, n).astype(BF16)
        parts = _split3(la)
        bcum = sum(_dot(tri_bf, pt) for pt in parts)
        b_last = bcum[c_rows - 1:c_rows, :]
        ref_pt = 0.5 * b_last
        q_in = (q * jnp.exp(bcum - ref_pt))
        k_in = (k * jnp.exp(ref_pt - bcum)).astype(BF16)
        q_st = (q * jnp.exp(bcum)).astype(BF16)
        k_st = (k * jnp.exp(b_last - bcum)).astype(BF16)
        dec = jnp.exp(sum(_dot_tn(pt, ones_bf) for pt in parts))
        dec = jnp.concatenate([dec] * HEADS, axis=1)
        state = s_ref[...]
        o_state = _dot(q_st, state.astype(BF16))
        for hd in range(HEADS):
            sl = slice(hd * HEAD_W, (hd + 1) * HEAD_W)
            qh = jnp.where(head_of_lane == hd, q_in, 0.0).astype(BF16)
            att = jnp.where(tri, _dot_nt(qh, k_in), 0.0)
            oh = _dot(att.astype(BF16), v[:, sl]) + o_state[:, sl]
            ob_ref[r0:r0 + n, sl] = _rms(oh, gob)[0:n]
        s_ref[...] = dec * state + jnp.where(diag_blocks, _dot_tn(k_st, v), 0.0)

    @pl.when(i == pl.num_programs(1) - 1)
    def _emit():
        for hd in range(HEADS):
            sout_ref[hd] = s_ref[hd * G_K:(hd + 1) * G_K, hd * HEAD_W:(hd + 1) * HEAD_W]


def _gla(gq, gk, gv, la, s0, g_ob, chunk, rows):
    b, t, _ = gq.shape
    spec = lambda c: pl.BlockSpec((None, rows, c), lambda bb, i: (bb, i, 0))
    st_spec = pl.BlockSpec((None, HEADS, G_K, HEAD_W), lambda bb, i: (bb, 0, 0, 0))
    return pl.pallas_call(
        functools.partial(_gla_kernel, chunk=chunk, rows=rows),
        grid=(b, t // rows),
        in_specs=[spec(G_KW), spec(G_KW), spec(WIDTH), spec(G_KW), st_spec,
                  pl.BlockSpec(g_ob.shape, lambda bb, i: (0, 0))],
        out_specs=[spec(WIDTH), st_spec],
        out_shape=[jax.ShapeDtypeStruct((b, t, WIDTH), F32),
                   jax.ShapeDtypeStruct((b, HEADS, G_K, HEAD_W), F32)],
        scratch_shapes=[pltpu.VMEM((G_KW, WIDTH), F32)],
        compiler_params=_params(("arbitrary", "arbitrary")),
        name="gla",
    )(gq, gk, gv, la, s0, g_ob)


def _memkv_kernel(mem_ref, nm_ref, w_ref, gkm_ref, mk_ref, mv_ref, mkb_ref, mvb_ref):
    hm = _rms(mem_ref[...], nm_ref[...]).astype(BF16)
    kv = _dot(hm, w_ref[...])
    gkm = gkm_ref[...]
    for hd in range(HEADS):
        sl = slice(hd * HEAD_W, (hd + 1) * HEAD_W)
        kn = _rms(kv[:, sl], gkm)
        mk_ref[:, sl] = kn
        mkb_ref[:, sl] = kn.astype(BF16)
    mv = kv[:, WIDTH:]
    mv_ref[...] = mv
    mvb_ref[...] = mv.astype(BF16)


def _memkv(mem2d, norm_mem, w_mkv, g_km, tm):
    n, d = mem2d.shape
    rows = lambda c: pl.BlockSpec((tm, c), lambda i: (i, 0))
    return pl.pallas_call(
        _memkv_kernel,
        grid=(n // tm,),
        in_specs=[rows(d), _const_spec(norm_mem.shape), _const_spec(w_mkv.shape),
                  _const_spec(g_km.shape)],
        out_specs=[rows(WIDTH)] * 4,
        out_shape=[jax.ShapeDtypeStruct((n, WIDTH), dt) for dt in (F32, F32, BF16, BF16)],
        compiler_params=_params(("arbitrary",)),
        name="memkv",
    )(mem2d, norm_mem, w_mkv, g_km)


def _memattn_kernel(mq_ref, mk_ref, mv_ref, om_ref):
    for hd in range(HEADS):
        sl = slice(hd * HEAD_W, (hd + 1) * HEAD_W)
        q = mq_ref[:, sl]
        if q.shape[0] % 16 == 0:
            q = q.astype(BF16)
            cast = lambda t: t.astype(BF16)
        else:
            q = q.astype(F32)
            cast = lambda t: t.astype(F32)
        s = _dot_nt(q, cast(mk_ref[:, sl]))
        pm = jnp.exp(s - jnp.max(s, axis=-1, keepdims=True))
        o = _dot(cast(pm), cast(mv_ref[:, sl])) / jnp.sum(pm, axis=-1, keepdims=True)
        om_ref[:, sl] = o


def _memattn(mq, mk, mv, tm):
    b, t, _ = mq.shape
    n_mem = mk.shape[1]
    rows = pl.BlockSpec((None, tm, WIDTH), lambda bb, i: (bb, i, 0))
    mem = pl.BlockSpec((None, n_mem, WIDTH), lambda bb, i: (bb, 0, 0))
    return pl.pallas_call(
        _memattn_kernel,
        grid=(b, t // tm),
        in_specs=[rows, mem, mem],
        out_specs=rows,
        out_shape=jax.ShapeDtypeStruct((b, t, WIDTH), F32),
        compiler_params=_params(("arbitrary", "arbitrary")),
        name="memattn",
    )(mq, mk, mv)


def _merge_kernel(x_ref, oa_ref, az_ref, ob_ref, gz_ref, om_ref, mz_ref, gl_ref, wb_ref, wo_ref,
                  y_ref):
    d = x_ref.shape[1]
    merged = None
    for br, (o_ref, z_ref) in enumerate(((oa_ref, az_ref), (ob_ref, gz_ref), (om_ref, mz_ref))):
        z = z_ref[...]
        u = _dot((o_ref[...] * (z * jax.nn.sigmoid(z))).astype(BF16), wb_ref[br])
        term = jax.nn.sigmoid(gl_ref[:, br * d:(br + 1) * d]) * u
        merged = term if merged is None else merged + term
    y_ref[...] = x_ref[...] + _dot(merged.astype(BF16), wo_ref[...])


def _merge(x2d, oa, az, ob, gz, om, mz, gl, w_branch, w_o, tm):
    n, d = x2d.shape
    rows = lambda c: pl.BlockSpec((tm, c), lambda i: (i, 0))
    return pl.pallas_call(
        _merge_kernel,
        grid=(n // tm,),
        in_specs=[rows(d)] + [rows(WIDTH)] * 6 + [rows(N_BRANCH * d), _const_spec(w_branch.shape),
                                                  _const_spec(w_o.shape)],
        out_specs=rows(d),
        out_shape=jax.ShapeDtypeStruct((n, d), F32),
        compiler_params=_params(("arbitrary",)),
        name="merge",
    )(x2d, oa, az, ob, gz, om, mz, gl, w_branch, w_o)


def _layer_weights(l, norm_x, w_in, g_qa, g_ka, g_oa, w_a2, b_a, g_ob, g_qm, g_km, norm_mem,
                   w_mkv, w_branch, w_o):
    d = w_in.shape[1]
    w = w_in[l]
    w_packed = jnp.concatenate(
        [w[:, :SRC_GR_END], jnp.zeros((d, OFF_MQ - OFF_GR - G_RANK), w.dtype), w[:, SRC_GR_END:]],
        axis=1).astype(BF16)
    grp = np.arange(WIDTH) // A_QK
    gmat = jnp.asarray((grp[:, None] == grp[None, :]).astype(np.float32) / A_QK, BF16)
    wa2 = jnp.concatenate([w_a2[l], jnp.zeros((LANE - G_RANK, G_KW), w_a2.dtype)], 0).astype(BF16)
    return {
        "norm_x": norm_x[l][None, :],
        "w_in": w_packed,
        "gmat": gmat,
        "g_qa": jnp.tile(g_qa[l], WIDTH // A_QK)[None, :],
        "g_ka": jnp.tile(g_ka[l], WIDTH // A_QK)[None, :],
        "g_qm": g_qm[l][None, :],
        "w_a2": wa2,
        "b_a": b_a[l][None, :],
        "g_oa": g_oa[l][None, :],
        "g_ob": g_ob[l][None, :],
        "g_km": g_km[l][None, :],
        "norm_mem": norm_mem[l][None, :],
        "w_mkv": w_mkv[l].astype(BF16),
        "w_branch": w_branch[l].astype(BF16),
        "w_o": w_o[l].astype(BF16),
    }


def _row_tile(n, want):
    t = min(n, want)
    assert n % t == 0
    return t


def kernel(x_prompt, x_sample, cache_k_a, cache_v_a, page_table, state_gla, cache_mem_k,
           cache_mem_v, mem_prompt, norm_x, w_in, g_qa, g_ka, lam, g_oa, w_a2, b_a, g_ob, g_qm,
           g_km, norm_mem, w_mkv, w_branch, w_o):
    depth = w_in.shape[0]
    bp, tp, d = x_prompt.shape
    bs, ts, _ = x_sample.shape
    n_mem = mem_prompt.shape[1]
    n_pool, page = cache_k_a.shape[1], cache_k_a.shape[2]
    wts = [_layer_weights(l, norm_x, w_in, g_qa, g_ka, g_oa, w_a2, b_a, g_ob, g_qm, g_km, norm_mem,
                          w_mkv, w_branch, w_o) for l in range(depth)]
    lam_inits = [0.8 - 0.6 * math.exp(-0.3 * l) for l in range(depth)]

    tq = _row_tile(tp, 512)
    gla_rows = _row_tile(tp, 512)
    gla_chunk = 64 if gla_rows % 64 == 0 else gla_rows
    x = x_prompt.reshape(bp * tp, d)
    kp, vp, sp, mkp, mvp = [], [], [], [], []
    for l in range(depth):
        w = wts[l]
        mk, mv, mkb, mvb = _memkv(mem_prompt.reshape(bp * n_mem, d), w["norm_mem"], w["w_mkv"],
                                  w["g_km"], _row_tile(bp * n_mem, 256))
        (q, k, v, kb, vb, az, gq, gk, gv, gz, la, mq, mz, gl) = _inproj(
            x, w, BF16, _row_tile(bp * tp, 256))
        r3 = lambda a: a.reshape(bp, tp, a.shape[-1])
        oa = _attn_prompt(r3(q), r3(kb), r3(vb), lam[l], w["g_oa"], lam_inits[l], tq, tq)
        ob, s_fin = _gla(r3(gq), r3(gk), r3(gv), r3(la),
                         jnp.zeros((bp, HEADS, G_K, HEAD_W), F32), w["g_ob"], gla_chunk, gla_rows)
        om = _memattn(r3(mq), mkb.reshape(bp, n_mem, WIDTH), mvb.reshape(bp, n_mem, WIDTH),
                      _row_tile(tp, 512))
        x = _merge(x, oa.reshape(-1, WIDTH), az, ob.reshape(-1, WIDTH), gz, om.reshape(-1, WIDTH),
                   mz, gl, w["w_branch"], w["w_o"], _row_tile(bp * tp, 512))
        kp.append(k.reshape(bp, tp, HEADS, HEAD_W))
        vp.append(v.reshape(bp, tp, HEADS, HEAD_W))
        sp.append(s_fin)
        mkp.append(mk.reshape(bp, n_mem, HEADS, HEAD_W))
        mvp.append(mv.reshape(bp, n_mem, HEADS, HEAD_W))
    y_prompt = x.reshape(bp, tp, d)

    x = x_sample.reshape(bs * ts, d)
    ks_, vs_, ss_ = [], [], []
    for l in range(depth):
        w = wts[l]
        (q, k, v, kb, vb, az, gq, gk, gv, gz, la, mq, mz, gl) = _inproj(
            x, w, F32, _row_tile(bs * ts, 256))
        r3 = lambda a: a.reshape(bs, ts, a.shape[-1])
        oa = _attn_sample(r3(q), r3(k), r3(v), lam[l], w["g_oa"],
                          cache_k_a[l].reshape(n_pool, page, WIDTH),
                          cache_v_a[l].reshape(n_pool, page, WIDTH), page_table, lam_inits[l])
        ob, s_fin = _gla(r3(gq), r3(gk), r3(gv), r3(la), state_gla[l], w["g_ob"], 64, ts)
        om = _memattn(r3(mq), cache_mem_k[l].reshape(bs, n_mem, WIDTH),
                      cache_mem_v[l].reshape(bs, n_mem, WIDTH), ts)
        x = _merge(x, oa.reshape(-1, WIDTH), az, ob.reshape(-1, WIDTH), gz, om.reshape(-1, WIDTH),
                   mz, gl, w["w_branch"], w["w_o"], _row_tile(bs * ts, 512))
        ks_.append(k.reshape(bs, ts, HEADS, HEAD_W))
        vs_.append(v.reshape(bs, ts, HEADS, HEAD_W))
        ss_.append(s_fin)
    y_sample = x.reshape(bs, ts, d)

    return (y_prompt, y_sample, jnp.stack(kp), jnp.stack(vp), jnp.stack(ks_), jnp.stack(vs_),
            jnp.stack(sp), jnp.stack(ss_), jnp.stack(mkp), jnp.stack(mvp))
```
